```python
import jax, jax.numpy as jnp
from jax import lax
import numpy as np

D_MODEL = 2048
BATCH = 2
SEQ = 8192
DEPTH = 4

CTX_LEN = 256
GRID_W = 64

MIX_WIDTH = D_MODEL
CONV_WIDTH = MIX_WIDTH // 4
CONV_K = 3
GLA_HEADS = 4
GLA_DV = MIX_WIDTH // 4 // GLA_HEADS
GLA_DK = GLA_DV // 2
GLA_LOWRANK = 16
GLA_TAU = 16.0
GLA_CHUNK = 64
SWA_HEADS = 8
SWA_KV_HEADS = 2
SWA_GROUP = SWA_HEADS // SWA_KV_HEADS
SWA_HEAD_DIM = MIX_WIDTH // 2 // SWA_HEADS
SWA_WINDOW = 128
SWA_BLOCK = 128
ROPE_THETA = 10000.0
NEG_INF = -1e30
PEER_HEADS = 8
PEER_NKEYS = 128
PEER_EXPERTS = PEER_NKEYS * PEER_NKEYS
PEER_DKEY = 256
PEER_TOPK = 16
PEER_TOKEN_BLOCK = 128
DEEPNORM_ALPHA = (2.0 * DEPTH) ** 0.25
DEEPNORM_BETA = (8.0 * DEPTH) ** -0.25
LN_EPS = 1e-6

IN_SPLITS = (
    CONV_WIDTH, CONV_WIDTH, CONV_WIDTH,
    GLA_HEADS * GLA_DK, GLA_HEADS * GLA_DK,
    GLA_HEADS * GLA_DV, GLA_HEADS * GLA_DV,
    GLA_LOWRANK, GLA_LOWRANK,
    SWA_HEADS * SWA_HEAD_DIM,
    SWA_KV_HEADS * SWA_HEAD_DIM, SWA_KV_HEADS * SWA_HEAD_DIM,
)
IN_WIDTH = sum(IN_SPLITS)

kernel_name = "hymba_conv_gla_swa_peer_deepnorm_dit"


def layer_norm(x, g, b):
    xf = x.astype(jnp.float32)
    mu = jnp.mean(xf, axis=-1, keepdims=True)
    var = jnp.mean(jnp.square(xf - mu), axis=-1, keepdims=True)
    return ((xf - mu) * lax.rsqrt(var + LN_EPS) * g + b).astype(x.dtype)


def split_cols(z):
    points, acc = [], 0
    for width in IN_SPLITS[:-1]:
        acc += width
        points.append(acc)
    return jnp.split(z, points, axis=-1)


def axial_rope_tables(n_tokens, dtype):
    rows = n_tokens // GRID_W
    r, col = jnp.meshgrid(jnp.arange(rows, dtype=jnp.float32), jnp.arange(GRID_W, dtype=jnp.float32), indexing='ij')
    axis_dim = SWA_HEAD_DIM // 2
    inv_freq = ROPE_THETA ** (-jnp.arange(0, axis_dim, 2, dtype=jnp.float32) / axis_dim)
    ang_r = r.reshape(-1, 1) * inv_freq
    ang_c = col.reshape(-1, 1) * inv_freq
    cos_r = jnp.cos(ang_r).astype(dtype)[None, :, None, :]
    sin_r = jnp.sin(ang_r).astype(dtype)[None, :, None, :]
    cos_c = jnp.cos(ang_c).astype(dtype)[None, :, None, :]
    sin_c = jnp.sin(ang_c).astype(dtype)[None, :, None, :]
    return (cos_r, sin_r, cos_c, sin_c)


def rotate_half(x, cos, sin):
    x1, x2 = jnp.split(x, 2, axis=-1)
    return jnp.concatenate([x1 * cos - x2 * sin, x2 * cos + x1 * sin], axis=-1)


def apply_axial_rope(x, tables):
    cos_r, sin_r, cos_c, sin_c = tables
    x_row, x_col = jnp.split(x, 2, axis=-1)
    return jnp.concatenate([rotate_half(x_row, cos_r, sin_r), rotate_half(x_col, cos_c, sin_c)], axis=-1)


def short_conv(z, w):
    return lax.conv_general_dilated(z, w[:, None, :], window_strides=(1,), padding=((CONV_K // 2, CONV_K // 2),),
                                    dimension_numbers=('NWC', 'WIO', 'NWC'), feature_group_count=z.shape[-1])


def conv_mixer(x_in, gate_b, gate_c, w):
    return gate_b * short_conv(gate_c * x_in, w)


def gla_heads(parts, w2, b2):
    _, _, _, q, k, v, g, lr_f, lr_b, _, _, _ = parts
    bsz, n = q.shape[:2]
    q = q.reshape(bsz, n, GLA_HEADS, GLA_DK) * GLA_DK ** -0.5
    k = k.reshape(bsz, n, GLA_HEADS, GLA_DK)
    v = v.reshape(bsz, n, GLA_HEADS, GLA_DV)
    lg_f = gla_log_decay(lr_f, w2[0], b2[0])
    lg_b = gla_log_decay(lr_b, w2[1], b2[1])
    return q, k, v, lg_f, lg_b, g


def gla_log_decay(lr, w2, b2):
    z = (lr @ w2 + b2).astype(jnp.float32)
    return (jax.nn.log_sigmoid(z) / GLA_TAU).reshape(lr.shape[0], lr.shape[1], GLA_HEADS, GLA_DK)


def gla_chunk_states(k, v, lg, s0):
    bsz, n_tok, h, dk = k.shape
    dv = v.shape[-1]
    n = n_tok // GLA_CHUNK
    kc = k.reshape(bsz, n, GLA_CHUNK, h, dk)
    vc = v.reshape(bsz, n, GLA_CHUNK, h, dv)
    b = jnp.cumsum(lg.reshape(bsz, n, GLA_CHUNK, h, dk), axis=2)
    b_last = b[:, :, -1]
    k_end = kc * jnp.exp(b_last[:, :, None] - b).astype(k.dtype)
    delta = jnp.einsum('bnchk,bnchv->nbhkv', k_end, vc)
    decay = jnp.exp(b_last).astype(k.dtype).transpose(1, 0, 2, 3)

    def step(s, inp):
        dec, dlt = inp
        return dec[..., None] * s + dlt, s

    s_final, s_starts = lax.scan(step, s0, (decay, delta))
    return s_starts, s_final, b


def gla_chunk_outputs(q, k, v, b, s_starts):
    bsz, n_tok, h, dk = q.shape
    dv = v.shape[-1]
    n = n_tok // GLA_CHUNK
    qe = q.reshape(bsz, n, GLA_CHUNK, h, dk) * jnp.exp(b).astype(q.dtype)
    ke = k.reshape(bsz, n, GLA_CHUNK, h, dk) * jnp.exp(-b).astype(k.dtype)
    vc = v.reshape(bsz, n, GLA_CHUNK, h, dv)
    att = jnp.einsum('bnihk,bnjhk->bnhij', qe, ke)
    lower_tri = jnp.tril(jnp.ones((GLA_CHUNK, GLA_CHUNK), dtype=bool))
    att = jnp.where(lower_tri, att, jnp.zeros_like(att))
    o = jnp.einsum('bnhij,bnjhv->bnihv', att, vc) + jnp.einsum('bnihk,nbhkv->bnihv', qe, s_starts)
    return o.reshape(bsz, n_tok, h, dv)


def gla_bidir(q, k, v, lg_f, lg_b, s0_f, s0_b):
    st_f, fin_f, b_f = gla_chunk_states(k, v, lg_f, s0_f)
    o_f = gla_chunk_outputs(q, k, v, b_f, st_f)
    qr, kr, vr = jnp.flip(q, axis=1), jnp.flip(k, axis=1), jnp.flip(v, axis=1)
    st_b, fin_b, b_b = gla_chunk_states(kr, vr, jnp.flip(lg_b, axis=1), s0_b)
    o_b = jnp.flip(gla_chunk_outputs(qr, kr, vr, b_b, st_b), axis=1)
    return o_f + o_b, fin_f, fin_b


def gla_final_states(k, v, lg_f, lg_b, s0):
    _, fin_f, _ = gla_chunk_states(k, v, lg_f, s0)
    _, fin_b, _ = gla_chunk_states(jnp.flip(k, axis=1), jnp.flip(v, axis=1), jnp.flip(lg_b, axis=1), s0)
    return fin_f, fin_b


def gla_finish(o, g, norm_w):
    of = o.astype(jnp.float32)
    of = of * lax.rsqrt(jnp.mean(of * of, axis=-1, keepdims=True) + LN_EPS) * norm_w
    bsz, n = o.shape[:2]
    return of.astype(o.dtype).reshape(bsz, n, GLA_HEADS * GLA_DV) * jax.nn.silu(g)


def swa_heads(parts):
    sq, sk, sv = parts[9], parts[10], parts[11]
    bsz, n = sq.shape[:2]
    return (sq.reshape(bsz, n, SWA_HEADS, SWA_HEAD_DIM),
            sk.reshape(bsz, n, SWA_KV_HEADS, SWA_HEAD_DIM),
            sv.reshape(bsz, n, SWA_KV_HEADS, SWA_HEAD_DIM))


def sink_column(sink, lead_shape):
    col = sink.astype(jnp.float32).reshape(SWA_KV_HEADS, SWA_GROUP)[:, :, None, None]
    return jnp.broadcast_to(col, lead_shape + (1,))


def swa_latent(q, k, v, k_ctx, v_ctx, sink):
    bsz, n = q.shape[:2]
    nb = n // SWA_BLOCK
    band = 3 * SWA_BLOCK
    qb = q.reshape(bsz, nb, SWA_BLOCK, SWA_KV_HEADS, SWA_GROUP, SWA_HEAD_DIM) * SWA_HEAD_DIM ** -0.5
    pad = ((0, 0), (SWA_BLOCK, SWA_BLOCK), (0, 0), (0, 0))
    band_idx = jnp.arange(nb)[:, None] * SWA_BLOCK + jnp.arange(band)[None, :]
    kb = jnp.pad(k, pad)[:, band_idx]
    vb = jnp.pad(v, pad)[:, band_idx]
    q_pos = jnp.arange(nb)[:, None] * SWA_BLOCK + jnp.arange(SWA_BLOCK)[None, :]
    k_pos = band_idx - SWA_BLOCK
    valid = ((jnp.abs(q_pos[:, :, None] - k_pos[:, None, :]) <= SWA_WINDOW)
             & (k_pos[:, None, :] >= 0) & (k_pos[:, None, :] < n))
    s_loc = jnp.einsum('bnqhgd,bnkhd->bnhgqk', qb, kb).astype(jnp.float32)
    s_loc = jnp.where(valid[None, :, None, None], s_loc, NEG_INF)
    s_ctx = jnp.einsum('bnqhgd,bchd->bnhgqc', qb, k_ctx).astype(jnp.float32)
    logits = jnp.concatenate([s_loc, s_ctx, sink_column(sink, s_loc.shape[:-1])], axis=-1)
    p = jax.nn.softmax(logits, axis=-1).astype(v.dtype)
    n_ctx = k_ctx.shape[1]
    o = (jnp.einsum('bnhgqk,bnkhd->bnqhgd', p[..., :band], vb)
         + jnp.einsum('bnhgqc,bchd->bnqhgd', p[..., band:band + n_ctx], v_ctx))
    return o.reshape(bsz, n, SWA_HEADS * SWA_HEAD_DIM)


def swa_context(q, k, v, sink):
    bsz, n = q.shape[:2]
    qg = q.reshape(bsz, n, SWA_KV_HEADS, SWA_GROUP, SWA_HEAD_DIM) * SWA_HEAD_DIM ** -0.5
    s = jnp.einsum('bqhgd,bkhd->bhgqk', qg, k).astype(jnp.float32)
    logits = jnp.concatenate([s, sink_column(sink, s.shape[:-1])], axis=-1)
    p = jax.nn.softmax(logits, axis=-1)[..., :n].astype(v.dtype)
    return jnp.einsum('bhgqk,bkhd->bqhgd', p, v).reshape(bsz, n, SWA_HEADS * SWA_HEAD_DIM)


def token_mixer(hl, hc, rope, w_in, conv_w, gla_w2, gla_b2, gla_norm, swa_sink, w_out, with_ctx_out):
    pl = split_cols(hl @ w_in)
    pc = split_cols(hc @ w_in)
    q_c, k_c, v_c, lgf_c, lgb_c, g_c = gla_heads(pc, gla_w2, gla_b2)
    sq_c, sk_c, sv_c = swa_heads(pc)
    s0 = jnp.zeros((hc.shape[0], GLA_HEADS, GLA_DK, GLA_DV), hc.dtype)
    if with_ctx_out:
        o_c, s_ctx_f, s_ctx_b = gla_bidir(q_c, k_c, v_c, lgf_c, lgb_c, s0, s0)
    else:
        s_ctx_f, s_ctx_b = gla_final_states(k_c, v_c, lgf_c, lgb_c, s0)
    q_l, k_l, v_l, lgf_l, lgb_l, g_l = gla_heads(pl, gla_w2, gla_b2)
    o_l, _, _ = gla_bidir(q_l, k_l, v_l, lgf_l, lgb_l, s_ctx_f, s_ctx_b)
    sq_l, sk_l, sv_l = swa_heads(pl)
    swa_l = swa_latent(apply_axial_rope(sq_l, rope), apply_axial_rope(sk_l, rope), sv_l, sk_c, sv_c, swa_sink)
    y_l = jnp.concatenate([conv_mixer(pl[0], pl[1], pl[2], conv_w), gla_finish(o_l, g_l, gla_norm), swa_l],
                          axis=-1) @ w_out
    if with_ctx_out:
        y_c = jnp.concatenate([conv_mixer(pc[0], pc[1], pc[2], conv_w), gla_finish(o_c, g_c, gla_norm),
                               swa_context(sq_c, sk_c, sv_c, swa_sink)], axis=-1) @ w_out
        return y_l, y_c
    return y_l, None


def peer_ffn(h, wq, subkeys, u_tab, v_tab):
    n_tok, d = h.shape
    q = (h @ wq).reshape(n_tok, PEER_HEADS, 2, PEER_DKEY // 2)
    s = jnp.einsum('thpd,pnd->thpn', q, subkeys).astype(jnp.float32)
    s_top, i_top = lax.top_k(s, PEER_TOPK)
    cand = (s_top[:, :, 0, :, None] + s_top[:, :, 1, None, :]).reshape(n_tok, PEER_HEADS, PEER_TOPK * PEER_TOPK)
    best, flat = lax.top_k(cand, PEER_TOPK)
    i1 = jnp.take_along_axis(i_top[:, :, 0], flat // PEER_TOPK, axis=-1)
    i2 = jnp.take_along_axis(i_top[:, :, 1], flat % PEER_TOPK, axis=-1)
    experts = i1 * PEER_NKEYS + i2
    gates = jax.nn.softmax(best, axis=-1).astype(h.dtype)
    nblk = n_tok // PEER_TOKEN_BLOCK

    def block(args):
        hb, eb, gb = args
        act = jax.nn.gelu(jnp.einsum('thkd,td->thk', u_tab[eb], hb), approximate=False)
        return jnp.einsum('thk,thkd->td', act * gb, v_tab[eb])

    y = lax.map(block, (h.reshape(nblk, PEER_TOKEN_BLOCK, d),
                        experts.reshape(nblk, PEER_TOKEN_BLOCK, PEER_HEADS, PEER_TOPK),
                        gates.reshape(nblk, PEER_TOKEN_BLOCK, PEER_HEADS, PEER_TOPK)))
    return y.reshape(n_tok, d)


def setup_inputs(seed: int = 0) -> dict:
    key = jax.random.key(seed)
    ks = jax.random.split(key, 19)
    f32 = jnp.float32

    def nrm(k, shape, scale):
        return jax.random.normal(k, shape, f32) * scale

    return {
        "x": nrm(ks[0], (BATCH, SEQ, D_MODEL), 1.0),
        "c": nrm(ks[1], (BATCH, D_MODEL), 1.0),
        "ctx": nrm(ks[2], (BATCH, CTX_LEN, D_MODEL), 1.0),
        "c_ctx": nrm(ks[3], (D_MODEL,), 1.0),
        "w_ada": nrm(ks[4], (DEPTH, D_MODEL, 6 * D_MODEL), D_MODEL ** -0.5),
        "b_ada": nrm(ks[5], (DEPTH, 6 * D_MODEL), 0.02),
        "w_in": nrm(ks[6], (DEPTH, D_MODEL, IN_WIDTH), D_MODEL ** -0.5),
        "conv_w": nrm(ks[7], (DEPTH, CONV_K, CONV_WIDTH), CONV_K ** -0.5),
        "gla_w2": nrm(ks[8], (DEPTH, 2, GLA_LOWRANK, GLA_HEADS * GLA_DK), GLA_LOWRANK ** -0.5),
        "gla_b2": nrm(ks[9], (DEPTH, 2, GLA_HEADS * GLA_DK), 0.1),
        "gla_norm": 1.0 + nrm(ks[10], (DEPTH, GLA_DV), 0.1),
        "swa_sink": nrm(ks[11], (DEPTH, SWA_HEADS), 0.5),
        "w_out": nrm(ks[12], (DEPTH, MIX_WIDTH, D_MODEL), MIX_WIDTH ** -0.5 * DEEPNORM_BETA),
        "ln_g": 1.0 + nrm(ks[13], (DEPTH, 2, D_MODEL), 0.1),
        "ln_b": nrm(ks[14], (DEPTH, 2, D_MODEL), 0.02),
        "peer_wq": nrm(ks[15], (DEPTH, D_MODEL, PEER_HEADS * PEER_DKEY), D_MODEL ** -0.5),
        "peer_subkeys": nrm(ks[16], (DEPTH, 2, PEER_NKEYS, PEER_DKEY // 2), (PEER_DKEY // 2) ** -0.5),
        "peer_u": nrm(ks[17], (DEPTH, PEER_EXPERTS, D_MODEL), D_MODEL ** -0.5),
        "peer_v": nrm(ks[18], (DEPTH, PEER_EXPERTS, D_MODEL), PEER_HEADS ** -0.5 * DEEPNORM_BETA),
    }


def reference(x, c, ctx, c_ctx, w_ada, b_ada, w_in, conv_w, gla_w2, gla_b2, gla_norm, swa_sink, w_out,
              ln_g, ln_b, peer_wq, peer_subkeys, peer_u, peer_v):
    xl, xc = x, ctx
    rope = axial_rope_tables(x.shape[1], x.dtype)
    for layer in range(DEPTH):
        last = layer == DEPTH - 1
        sh1, sc1, g1, sh2, sc2, g2 = jnp.split((jax.nn.silu(c) @ w_ada[layer] + b_ada[layer])[:, None, :], 6, axis=-1)
        csh1, csc1, cg1, csh2, csc2, cg2 = jnp.split(jax.nn.silu(c_ctx) @ w_ada[layer] + b_ada[layer], 6, axis=-1)
        ml, mc = token_mixer(xl * (1 + sc1) + sh1, xc * (1 + csc1) + csh1, rope, w_in[layer], conv_w[layer],
                             gla_w2[layer], gla_b2[layer], gla_norm[layer], swa_sink[layer], w_out[layer],
                             not last)
        xl = layer_norm(DEEPNORM_ALPHA * xl + g1 * ml, ln_g[layer, 0], ln_b[layer, 0])
        hl = (xl * (1 + sc2) + sh2).reshape(-1, xl.shape[-1])
        fl = peer_ffn(hl, peer_wq[layer], peer_subkeys[layer], peer_u[layer], peer_v[layer]).reshape(xl.shape)
        xl = layer_norm(DEEPNORM_ALPHA * xl + g2 * fl, ln_g[layer, 1], ln_b[layer, 1])
        if not last:
            xc = layer_norm(DEEPNORM_ALPHA * xc + cg1 * mc, ln_g[layer, 0], ln_b[layer, 0])
            hc = (xc * (1 + csc2) + csh2).reshape(-1, xc.shape[-1])
            fc = peer_ffn(hc, peer_wq[layer], peer_subkeys[layer], peer_u[layer], peer_v[layer]).reshape(xc.shape)
            xc = layer_norm(DEEPNORM_ALPHA * xc + cg2 * fc, ln_g[layer, 1], ln_b[layer, 1])
    return xl
```

```python
import functools

import numpy as np
import jax
import jax.numpy as jnp
from jax import lax
from jax.experimental import pallas as pl
from jax.experimental.pallas import tpu as pltpu

F32 = jnp.float32
BF16 = jnp.bfloat16
HIGHEST = lax.Precision.HIGHEST

LANE = 128
SUBLANE = 8
VMEM_LIMIT = 56 * 1024 * 1024

GRID_W = 64
CONV_K = 3
GLA_HEADS = 4
GLA_LOWRANK = 16
GLA_TAU = 16.0
GLA_CHUNK = 64
SWA_HEADS = 8
SWA_KV_HEADS = 2
SWA_GROUP = SWA_HEADS // SWA_KV_HEADS
SWA_BLOCK = 128
ROPE_THETA = 10000.0
NEG_INF = -1e30
PEER_HEADS = 8
PEER_NKEYS = 128
PEER_TOPK = 16
LN_EPS = 1e-6

NT_DIMS = (((1,), (1,)), ((), ()))


def _dot(a, b, precision=None):
    return jnp.dot(a, b, preferred_element_type=F32, precision=precision)


def _dot_nt(a, b, precision=None):
    return lax.dot_general(a, b, NT_DIMS, preferred_element_type=F32, precision=precision)


def _params(*sem):
    return pltpu.CompilerParams(dimension_semantics=sem, vmem_limit_bytes=VMEM_LIMIT)


def _ada_kernel(c_ref, w_ref, b_ref, o_ref):
    c = c_ref[...]
    s = c * jax.nn.sigmoid(c)
    o_ref[0] = _dot(s, w_ref[0], HIGHEST) + b_ref[0]


def _ada_all(cv, w_ada, b_ada, tn=1024):
    depth, d, n6 = w_ada.shape
    r = cv.shape[0]
    return pl.pallas_call(
        _ada_kernel,
        out_shape=jax.ShapeDtypeStruct((depth, r, n6), F32),
        grid=(depth, n6 // tn),
        in_specs=[pl.BlockSpec((r, d), lambda l, j: (0, 0)),
                  pl.BlockSpec((1, d, tn), lambda l, j: (l, 0, j)),
                  pl.BlockSpec((1, 1, tn), lambda l, j: (l, 0, j))],
        out_specs=pl.BlockSpec((1, r, tn), lambda l, j: (l, 0, j)),
        compiler_params=_params("arbitrary", "arbitrary"),
        name="ada_mod",
    )(cv, w_ada, b_ada.reshape(depth, 1, n6))


def _inproj_kernel(x_ref, mod_ref, w_ref, cos_ref, sin_ref, o_ref, *, n_rope):
    h = x_ref[...] * (1.0 + mod_ref[0, 1:2, :]) + mod_ref[0, 0:1, :]
    z = _dot(h.astype(BF16), w_ref[...])
    if n_rope:
        cos = cos_ref[...]
        sin = sin_ref[...]
        lane = lax.broadcasted_iota(jnp.int32, cos.shape, 1)
        first = (lane % 64) < 32
        for g in range(n_rope):
            zg = z[:, g * LANE:(g + 1) * LANE]
            sw = jnp.where(first, pltpu.roll(zg, 96, 1), pltpu.roll(zg, 32, 1))
            o_ref[:, g * LANE:(g + 1) * LANE] = zg * cos + sw * sin
        o_ref[:, n_rope * LANE:] = z[:, n_rope * LANE:]
    else:
        o_ref[...] = z


def _inproj(x, mod, w, cos, sin, seg_of, n_rope, tm=256):
    t, d = x.shape
    n = w.shape[1]
    return pl.pallas_call(
        functools.partial(_inproj_kernel, n_rope=n_rope),
        out_shape=jax.ShapeDtypeStruct((t, n), F32),
        grid=(t // tm,),
        in_specs=[pl.BlockSpec((tm, d), lambda i: (i, 0)),
                  pl.BlockSpec((1, SUBLANE, d), lambda i: (seg_of(i, tm), 0, 0)),
                  pl.BlockSpec((d, n), lambda i: (0, 0)),
                  pl.BlockSpec((tm, LANE), lambda i: (i, 0)),
                  pl.BlockSpec((tm, LANE), lambda i: (i, 0))],
        out_specs=pl.BlockSpec((tm, n), lambda i: (i, 0)),
        compiler_params=_params("arbitrary"),
        name="inproj_rope" if n_rope else "inproj",
    )(x, mod, w, cos, sin)


def _conv_kernel(x_ref, b_ref, c_ref, xp_ref, cp_ref, xn_ref, cn_ref, w_ref, o_ref, *, tm, n_lat, seq, n_ctx_seq):
    i = pl.program_id(0)
    r0 = i * tm
    in_lat = r0 < n_lat
    off = jnp.where(in_lat, r0 % seq, (r0 - n_lat) % n_ctx_seq)
    length = jnp.where(in_lat, seq, n_ctx_seq)
    has_prev = (off > 0).astype(F32)
    has_next = (off + tm < length).astype(F32)
    u = c_ref[...] * x_ref[...]
    u_prev = cp_ref[SUBLANE - 1:SUBLANE, :] * xp_ref[SUBLANE - 1:SUBLANE, :] * has_prev
    u_next = cn_ref[0:1, :] * xn_ref[0:1, :] * has_next
    row = lax.broadcasted_iota(jnp.int32, u.shape, 0)
    up = jnp.where(row == 0, u_prev, pltpu.roll(u, 1, 0))
    un = jnp.where(row == tm - 1, u_next, pltpu.roll(u, tm - 1, 0))
    w = w_ref[...]
    o_ref[...] = b_ref[...] * (w[0:1, :] * up + w[1:2, :] * u + w[2:3, :] * un)


def _conv(za, conv_w8, n_lat, seq, n_ctx_seq, tm=256):
    t = za.shape[0]
    cw = conv_w8.shape[1]
    nb8 = t // SUBLANE
    r = tm // SUBLANE
    prev = lambda i: jnp.maximum(i * r - 1, 0)
    nxt = lambda i: jnp.minimum((i + 1) * r, nb8 - 1)
    return pl.pallas_call(
        functools.partial(_conv_kernel, tm=tm, n_lat=n_lat, seq=seq, n_ctx_seq=n_ctx_seq),
        out_shape=jax.ShapeDtypeStruct((t, cw), F32),
        grid=(t // tm,),
        in_specs=[pl.BlockSpec((tm, cw), lambda i: (i, 0)),
                  pl.BlockSpec((tm, cw), lambda i: (i, 1)),
                  pl.BlockSpec((tm, cw), lambda i: (i, 2)),
                  pl.BlockSpec((SUBLANE, cw), lambda i: (prev(i), 0)),
                  pl.BlockSpec((SUBLANE, cw), lambda i: (prev(i), 2)),
                  pl.BlockSpec((SUBLANE, cw), lambda i: (nxt(i), 0)),
                  pl.BlockSpec((SUBLANE, cw), lambda i: (nxt(i), 2)),
                  pl.BlockSpec((SUBLANE, cw), lambda i: (0, 0))],
        out_specs=pl.BlockSpec((tm, cw), lambda i: (i, 0)),
        compiler_params=_params("arbitrary"),
        name="conv_mixer",
    )(za, za, za, za, za, za, za, conv_w8)


def _log_sigmoid(z):
    return jnp.minimum(z, 0.0) - jnp.log1p(jnp.exp(-jnp.abs(z)))


def _gla_kernel(*refs, tile, reverse, finish):
    if finish:
        q_ref, k_ref, v_ref, g_ref, lr_ref, w2_ref, b2_ref, nw_ref, of_ref, o_ref, st_ref = refs
    else:
        q_ref, k_ref, v_ref, lr_ref, w2_ref, b2_ref, o_ref, st_ref = refs
    dk = q_ref.shape[1] // GLA_HEADS
    dv = v_ref.shape[1] // GLA_HEADS
    n_chunks = tile // GLA_CHUNK

    @pl.when(pl.program_id(1) == 0)
    def _():
        st_ref[...] = jnp.zeros_like(st_ref)

    z = _dot(lr_ref[...], w2_ref[...], HIGHEST) + b2_ref[0:1, :]
    lg = _log_sigmoid(z) * (1.0 / GLA_TAU)
    row = lax.broadcasted_iota(jnp.int32, (tile, tile), 0)
    col = lax.broadcasted_iota(jnp.int32, (tile, tile), 1)
    same = (row // GLA_CHUNK) == (col // GLA_CHUNK)
    tri = jnp.where(same & ((col >= row) if reverse else (col <= row)), 1.0, 0.0).astype(F32)
    bcum = _dot(tri, lg, HIGHEST)
    ci = lax.broadcasted_iota(jnp.int32, (GLA_CHUNK, GLA_CHUNK), 0)
    cj = lax.broadcasted_iota(jnp.int32, (GLA_CHUNK, GLA_CHUNK), 1)
    keep = (cj >= ci) if reverse else (cj <= ci)
    scale = dk ** -0.5
    order = range(n_chunks - 1, -1, -1) if reverse else range(n_chunks)
    for c in order:
        rows = slice(c * GLA_CHUNK, (c + 1) * GLA_CHUNK)
        for h in range(GLA_HEADS):
            kc = slice(h * dk, (h + 1) * dk)
            vc = slice(h * dv, (h + 1) * dv)
            bq = bcum[rows, kc]
            blast = bq[0:1, :] if reverse else bq[GLA_CHUNK - 1:GLA_CHUNK, :]
            qh = q_ref[rows, kc] * scale
            kh = k_ref[rows, kc]
            vf = v_ref[rows, vc]
            vh = vf.astype(BF16)
            qe = (qh * jnp.exp(bq)).astype(BF16)
            ke = (kh * jnp.exp(-bq)).astype(BF16)
            kend = (kh * jnp.exp(blast - bq)).astype(BF16)
            att = jnp.where(keep, _dot_nt(qe, ke), 0.0)
            st = st_ref[h]
            o = _dot(att.astype(BF16), vh) + _dot_nt(qe, st.astype(BF16))
            st_ref[h] = st * jnp.exp(blast) + _dot(vf.T.astype(BF16), kend)
            o_ref[rows, vc] = o
    if finish:
        tot = o_ref[...] + of_ref[...]
        g = g_ref[...]
        gate = g * jax.nn.sigmoid(g)
        nw = nw_ref[0:1, :]
        for h in range(GLA_HEADS):
            vc = slice(h * dv, (h + 1) * dv)
            oh = tot[:, vc]
            oh = oh * lax.rsqrt(jnp.mean(oh * oh, axis=-1, keepdims=True) + LN_EPS) * nw
            o_ref[:, vc] = oh * gate[:, vc]


def _gla(za, w2p, b2p, nw8, o_f, *, batch, seq, n_ctx_seq, reverse, tile=256):
    t = za.shape[0]
    nlt = seq // tile
    nct = n_ctx_seq // tile
    base_ctx = batch * nlt
    dq = GLA_HEADS * 64
    dvw = GLA_HEADS * 128
    finish = o_f is not None

    def rb(b, j):
        if reverse:
            return jnp.where(j < nct, base_ctx + b * nct + (nct - 1 - j), b * nlt + (nlt - 1 - (j - nct)))
        return jnp.where(j < nct, base_ctx + b * nct + j, b * nlt + (j - nct))

    q_off, k_off, v_off, g_off, lr_off = 1536 // dq, 1792 // dq, 2048 // dvw, 2560 // dvw, 3072 // LANE
    specs = [pl.BlockSpec((tile, dq), lambda b, j: (rb(b, j), q_off)),
             pl.BlockSpec((tile, dq), lambda b, j: (rb(b, j), k_off)),
             pl.BlockSpec((tile, dvw), lambda b, j: (rb(b, j), v_off))]
    args = [za, za, za]
    if finish:
        specs.append(pl.BlockSpec((tile, dvw), lambda b, j: (rb(b, j), g_off)))
        args.append(za)
    specs += [pl.BlockSpec((tile, LANE), lambda b, j: (rb(b, j), lr_off)),
              pl.BlockSpec((LANE, dq), lambda b, j: (0, 0)),
              pl.BlockSpec((SUBLANE, dq), lambda b, j: (0, 0))]
    args += [za, w2p, b2p]
    if finish:
        specs += [pl.BlockSpec((SUBLANE, LANE), lambda b, j: (0, 0)),
                  pl.BlockSpec((tile, dvw), lambda b, j: (rb(b, j), 0))]
        args += [nw8, o_f]
    return pl.pallas_call(
        functools.partial(_gla_kernel, tile=tile, reverse=reverse, finish=finish),
        out_shape=jax.ShapeDtypeStruct((t, dvw), F32),
        grid=(batch, nct + nlt),
        in_specs=specs,
        out_specs=pl.BlockSpec((tile, dvw), lambda b, j: (rb(b, j), 0)),
        scratch_shapes=[pltpu.VMEM((GLA_HEADS, 128, 64), F32)],
        compiler_params=_params("arbitrary", "arbitrary"),
        name="gla_bwd_finish" if reverse else "gla_fwd",
    )(*args)


def _swa_kernel(q_ref, kp_ref, kc_ref, kn_ref, vp_ref, vc_ref, vn_ref, kx_ref, vx_ref, sink_ref, o_ref, *, nb):
    n = pl.program_id(2)
    blk = SWA_BLOCK
    hd = kc_ref.shape[1]
    n_ctx = kx_ref.shape[0]
    scale = hd ** -0.5
    q = jnp.concatenate([q_ref[:, g * hd:(g + 1) * hd] * scale for g in range(SWA_GROUP)], axis=0).astype(BF16)
    k = jnp.concatenate([kp_ref[...], kc_ref[...], kn_ref[...], kx_ref[...]], axis=0).astype(BF16)
    v = jnp.concatenate([vp_ref[...], vc_ref[...], vn_ref[...], vx_ref[...]], axis=0).astype(BF16)
    s = _dot_nt(q, k)
    qi = lax.broadcasted_iota(jnp.int32, s.shape, 0) % blk
    kj = lax.broadcasted_iota(jnp.int32, s.shape, 1)
    is_lat = n < nb
    lo = jnp.where(is_lat, jnp.where(n > 0, 0, blk), 3 * blk)
    hi = jnp.where(n < nb - 1, 3 * blk - 1, 2 * blk - 1)
    valid = (kj >= 3 * blk) | ((kj >= jnp.maximum(qi, lo)) & (kj <= jnp.minimum(qi + 2 * blk, hi)))
    s = jnp.where(valid, s, NEG_INF)
    for g in range(SWA_GROUP):
        sg = s[g * blk:(g + 1) * blk, :]
        sink = sink_ref[0, g:g + 1, 0:1]
        m = jnp.maximum(jnp.max(sg, axis=-1, keepdims=True), sink)
        p = jnp.exp(sg - m)
        den = jnp.sum(p, axis=-1, keepdims=True) + jnp.exp(sink - m)
        o_ref[:, g * hd:(g + 1) * hd] = _dot(p.astype(BF16), v) / den


def _swa(zb, sink_tab, *, batch, seq, n_ctx_seq, with_ctx):
    t = zb.shape[0]
    blk = SWA_BLOCK
    hd = 128
    nb = seq // blk
    ncb = n_ctx_seq // blk
    gw = SWA_GROUP * hd
    k_off = SWA_HEADS
    v_off = SWA_HEADS + SWA_KV_HEADS
    ctx_base = (batch * seq) // n_ctx_seq

    def qrow(b, n):
        return jnp.where(n < nb, b * nb + n, batch * nb + b * ncb + (n - nb))

    def lrow(b, n, d):
        return b * nb + jnp.clip(n + d, 0, nb - 1)

    specs = [pl.BlockSpec((blk, gw), lambda b, h, n: (qrow(b, n), h))]
    for off in (k_off, v_off):
        for d in (-1, 0, 1):
            specs.append(pl.BlockSpec((blk, hd), functools.partial(
                lambda b, h, n, d, off: (lrow(b, n, d), off + h), d=d, off=off)))
    specs += [pl.BlockSpec((n_ctx_seq, hd), lambda b, h, n: (ctx_base + b, k_off + h)),
              pl.BlockSpec((n_ctx_seq, hd), lambda b, h, n: (ctx_base + b, v_off + h)),
              pl.BlockSpec((1, SUBLANE, LANE), lambda b, h, n: (h, 0, 0))]
    rows = t if with_ctx else batch * seq
    return pl.pallas_call(
        functools.partial(_swa_kernel, nb=nb),
        out_shape=jax.ShapeDtypeStruct((rows, SWA_HEADS * hd), F32),
        grid=(batch, SWA_KV_HEADS, nb + (ncb if with_ctx else 0)),
        in_specs=specs,
        out_specs=pl.BlockSpec((blk, gw), lambda b, h, n: (qrow(b, n), h)),
        compiler_params=_params("arbitrary", "arbitrary", "arbitrary"),
        name="swa",
    )(zb, zb, zb, zb, zb, zb, zb, zb, zb, sink_tab)


def _layer_norm(r, g, b):
    mu = jnp.mean(r, axis=-1, keepdims=True)
    rc = r - mu
    var = jnp.mean(rc * rc, axis=-1, keepdims=True)
    return rc * lax.rsqrt(var + LN_EPS) * g + b


def _outproj_kernel(a_ref, g_ref, s_ref, x_ref, mod_ref, w_ref, ln_ref, o_ref, *, alpha):
    ca = a_ref.shape[1]
    cg = g_ref.shape[1]
    y = (_dot(a_ref[...].astype(BF16), w_ref[0:ca, :])
         + _dot(g_ref[...].astype(BF16), w_ref[ca:ca + cg, :])
         + _dot(s_ref[...].astype(BF16), w_ref[ca + cg:, :]))
    r = alpha * x_ref[...] + mod_ref[0, 2:3, :] * y
    o_ref[...] = _layer_norm(r, ln_ref[0:1, :], ln_ref[1:2, :])


def _outproj(conv_o, gla_o, swa_o, x, mod, w, ln8, seg_of, rows, alpha, tm=256):
    d = x.shape[1]
    return pl.pallas_call(
        functools.partial(_outproj_kernel, alpha=alpha),
        out_shape=jax.ShapeDtypeStruct((rows, d), F32),
        grid=(rows // tm,),
        in_specs=[pl.BlockSpec((tm, conv_o.shape[1]), lambda i: (i, 0)),
                  pl.BlockSpec((tm, gla_o.shape[1]), lambda i: (i, 0)),
                  pl.BlockSpec((tm, swa_o.shape[1]), lambda i: (i, 0)),
                  pl.BlockSpec((tm, d), lambda i: (i, 0)),
                  pl.BlockSpec((1, SUBLANE, d), lambda i: (seg_of(i, tm), 0, 0)),
                  pl.BlockSpec(w.shape, lambda i: (0, 0)),
                  pl.BlockSpec((SUBLANE, d), lambda i: (0, 0))],
        out_specs=pl.BlockSpec((tm, d), lambda i: (i, 0)),
        compiler_params=_params("arbitrary"),
        name="outproj_ln",
    )(conv_o, gla_o, swa_o, x, mod, w, ln8)


def _top_rows(cur, count, removed):
    tops = []
    for r in range(count):
        m = jnp.max(cur, axis=0, keepdims=True)
        tops.append(m)
        if r + 1 < count:
            cur = jnp.where(cur == m, removed, cur)
    return tops


def _peer_query_kernel(x_ref, mod_ref, wq_ref, sk_ref, hb_ref, e1_ref, e2_ref, th_ref, qt_ref):
    tm = x_ref.shape[0]
    nk = PEER_NKEYS
    h2 = (x_ref[...] * (1.0 + mod_ref[0, 4:5, :]) + mod_ref[0, 3:4, :]).astype(BF16)
    hb_ref[...] = h2
    qt_ref[...] = _dot_nt(wq_ref[...], h2)

    def head(h, carry):
        for c in range(tm // LANE):
            lanes = slice(c * LANE, (c + 1) * LANE)
            base = pl.multiple_of(h * 2 * nk, 2 * nk)
            s1 = _dot(sk_ref[0], qt_ref[pl.ds(base, nk), lanes], HIGHEST)
            s2 = _dot(sk_ref[1], qt_ref[pl.ds(base + nk, nk), lanes], HIGHEST)
            a = _top_rows(s1, PEER_TOPK, -jnp.inf)
            b = _top_rows(s2, PEER_TOPK, -jnp.inf)
            ea = [jnp.exp(r - a[0]) for r in a]
            eb = jnp.concatenate([jnp.exp(r - b[0]) for r in b], axis=0)
            cand = jnp.concatenate([r * eb for r in ea], axis=0)
            theta = _top_rows(cand, PEER_TOPK, -1.0)[-1]
            picked = cand >= theta
            rz = 1.0 / jnp.sum(jnp.where(picked, cand, 0.0), axis=0, keepdims=True)
            ebn = eb * rz
            candn = jnp.concatenate([r * ebn for r in ea], axis=0)
            th_ref[h, :, lanes] = jnp.min(jnp.where(picked, candn, jnp.inf), axis=0, keepdims=True)
            e1_ref[h, :, lanes] = jnp.exp(s1 - a[0])
            e2_ref[h, :, lanes] = jnp.exp(s2 - b[0]) * rz
        return carry

    lax.fori_loop(0, PEER_HEADS, head, 0)


def _peer_query(x1, mod, wqt, subkeys, seg_of, rows, tm=256):
    d = x1.shape[1]
    nq = wqt.shape[0]
    return pl.pallas_call(
        _peer_query_kernel,
        out_shape=(jax.ShapeDtypeStruct((rows, d), BF16),
                   jax.ShapeDtypeStruct((PEER_HEADS, PEER_NKEYS, rows), F32),
                   jax.ShapeDtypeStruct((PEER_HEADS, PEER_NKEYS, rows), F32),
                   jax.ShapeDtypeStruct((PEER_HEADS, 1, rows), F32)),
        grid=(rows // tm,),
        in_specs=[pl.BlockSpec((tm, d), lambda i: (i, 0)),
                  pl.BlockSpec((1, SUBLANE, d), lambda i: (seg_of(i, tm), 0, 0)),
                  pl.BlockSpec(wqt.shape, lambda i: (0, 0)),
                  pl.BlockSpec(subkeys.shape, lambda i: (0, 0, 0))],
        out_specs=(pl.BlockSpec((tm, d), lambda i: (i, 0)),
                   pl.BlockSpec((PEER_HEADS, PEER_NKEYS, tm), lambda i: (0, 0, i)),
                   pl.BlockSpec((PEER_HEADS, PEER_NKEYS, tm), lambda i: (0, 0, i)),
                   pl.BlockSpec((PEER_HEADS, 1, tm), lambda i: (0, 0, i))),
        scratch_shapes=[pltpu.VMEM((nq, tm), F32)],
        compiler_params=_params("arbitrary"),
        name="peer_query",
    )(x1, mod, wqt, subkeys)


def _gelu(a):
    return a * (lax.erf(a * (2.0 ** -0.5)) + 1.0) * 0.5


def _peer_dense_kernel(hb_ref, e1_ref, e2_ref, th_ref, u_ref, vt_ref, x_ref, mod_ref, ln_ref, o_ref,
                       acc_ref, p_ref, *, alpha, n1):
    j = pl.program_id(1)
    nk = PEER_NKEYS

    @pl.when(j == 0)
    def _():
        acc_ref[...] = jnp.zeros_like(acc_ref)

    a_t = _dot_nt(u_ref[...], hb_ref[...])
    for r in range(n1):
        gates = None
        for h in range(PEER_HEADS):
            w = e1_ref[h, 0, r:r + 1, :] * e2_ref[h]
            sel = jnp.where(w >= th_ref[h], w, 0.0)
            gates = sel if gates is None else gates + sel
        p_ref[r * nk:(r + 1) * nk, :] = (_gelu(a_t[r * nk:(r + 1) * nk, :]) * gates).astype(BF16)
    acc_ref[...] += _dot(vt_ref[...], p_ref[...])

    @pl.when(j == pl.num_programs(1) - 1)
    def _():
        f = acc_ref[...].T
        r_ = alpha * x_ref[...] + mod_ref[0, 5:6, :] * f
        o_ref[...] = _layer_norm(r_, ln_ref[2:3, :], ln_ref[3:4, :])


def _peer_dense(hb, e1, e2, th, u, vt, x1, mod, ln8, seg_of, rows, alpha, tm=512, n1=4):
    d = x1.shape[1]
    n_exp = u.shape[0]
    te = n1 * PEER_NKEYS
    e1g = e1.reshape(PEER_HEADS, PEER_NKEYS // n1, n1, rows)
    return pl.pallas_call(
        functools.partial(_peer_dense_kernel, alpha=alpha, n1=n1),
        out_shape=jax.ShapeDtypeStruct((rows, d), F32),
        grid=(rows // tm, n_exp // te),
        in_specs=[pl.BlockSpec((tm, d), lambda i, j: (i, 0)),
                  pl.BlockSpec((PEER_HEADS, 1, n1, tm), lambda i, j: (0, j, 0, i)),
                  pl.BlockSpec((PEER_HEADS, PEER_NKEYS, tm), lambda i, j: (0, 0, i)),
                  pl.BlockSpec((PEER_HEADS, 1, tm), lambda i, j: (0, 0, i)),
                  pl.BlockSpec((te, d), lambda i, j: (j, 0)),
                  pl.BlockSpec((d, te), lambda i, j: (0, j)),
                  pl.BlockSpec((tm, d), lambda i, j: (i, 0)),
                  pl.BlockSpec((1, SUBLANE, d), lambda i, j: (seg_of(i, tm), 0, 0)),
                  pl.BlockSpec((SUBLANE, d), lambda i, j: (0, 0))],
        out_specs=pl.BlockSpec((tm, d), lambda i, j: (i, 0)),
        scratch_shapes=[pltpu.VMEM((d, tm), F32), pltpu.VMEM((te, tm), BF16)],
        compiler_params=_params("arbitrary", "arbitrary"),
        name="peer_dense",
    )(hb, e1g, e2, th, u, vt, x1, mod, ln8)


def _rope_tables(seq, n_ctx_rows, batch, hd):
    rows = seq // GRID_W
    r, col = jnp.meshgrid(jnp.arange(rows, dtype=F32), jnp.arange(GRID_W, dtype=F32), indexing="ij")
    axis_dim = hd // 2
    inv_freq = ROPE_THETA ** (-jnp.arange(0, axis_dim, 2, dtype=F32) / axis_dim)
    ang_r = r.reshape(-1, 1) * inv_freq
    ang_c = col.reshape(-1, 1) * inv_freq
    cos = jnp.concatenate([jnp.cos(ang_r), jnp.cos(ang_r), jnp.cos(ang_c), jnp.cos(ang_c)], axis=-1)
    sin = jnp.concatenate([-jnp.sin(ang_r), jnp.sin(ang_r), -jnp.sin(ang_c), jnp.sin(ang_c)], axis=-1)
    cos = jnp.concatenate([jnp.tile(cos, (batch, 1)), jnp.ones((n_ctx_rows, hd), F32)], axis=0)
    sin = jnp.concatenate([jnp.tile(sin, (batch, 1)), jnp.zeros((n_ctx_rows, hd), F32)], axis=0)
    return cos, sin


def _pad_rows(a, rows):
    return jnp.pad(a, ((0, rows - a.shape[0]),) + ((0, 0),) * (a.ndim - 1))


def kernel(x, c, ctx, c_ctx, w_ada, b_ada, w_in, conv_w, gla_w2, gla_b2, gla_norm, swa_sink, w_out, ln_g, ln_b,
           peer_wq, peer_subkeys, peer_u, peer_v):
    batch, seq, d = x.shape
    n_ctx_seq = ctx.shape[1]
    depth = w_ada.shape[0]
    n_lat = batch * seq
    n_ctx = batch * n_ctx_seq
    t = n_lat + n_ctx
    alpha = (2.0 * depth) ** 0.25
    assert d == 2048 and seq % 512 == 0 and n_ctx_seq % 256 == 0 and n_lat % n_ctx_seq == 0 and n_ctx % 512 == 0

    def seg_of(i, tm):
        return jnp.minimum((i * tm) // seq, batch)

    n_seg = batch + 1
    cv = _pad_rows(jnp.concatenate([c, c_ctx[None, :]], axis=0), SUBLANE * pl.cdiv(n_seg, SUBLANE))
    mods = _ada_all(cv, w_ada, b_ada)[:, :n_seg]
    mods = mods.reshape(depth, n_seg, 6, d)
    mods = jnp.pad(mods, ((0, 0), (0, 0), (0, SUBLANE - 6), (0, 0)))

    cos, sin = _rope_tables(seq, n_ctx, batch, 128)
    xs = jnp.concatenate([x.reshape(n_lat, d), ctx.reshape(n_ctx, d)], axis=0)

    n_a = 3072 + 2 * GLA_LOWRANK
    for l in range(depth):
        last = l == depth - 1
        rows = n_lat if last else t
        w_a = jnp.pad(w_in[l, :, :n_a], ((0, 0), (0, LANE - 2 * GLA_LOWRANK))).astype(BF16)
        w_b = w_in[l, :, n_a:].astype(BF16)
        za = _inproj(xs, mods[l], w_a, cos, sin, seg_of, 0)
        zb = _inproj(xs, mods[l], w_b, cos, sin, seg_of, SWA_HEADS + SWA_KV_HEADS)

        conv_o = _conv(za, _pad_rows(conv_w[l], SUBLANE), n_lat, seq, n_ctx_seq)

        w2f = _pad_rows(gla_w2[l, 0], LANE)
        w2b = jnp.pad(gla_w2[l, 1], ((GLA_LOWRANK, LANE - 2 * GLA_LOWRANK), (0, 0)))
        b2f = _pad_rows(gla_b2[l, 0][None, :], SUBLANE)
        b2b = _pad_rows(gla_b2[l, 1][None, :], SUBLANE)
        nw8 = _pad_rows(gla_norm[l][None, :], SUBLANE)
        gla_kw = dict(batch=batch, seq=seq, n_ctx_seq=n_ctx_seq)
        o_f = _gla(za, w2f, b2f, None, None, reverse=False, **gla_kw)
        gla_o = _gla(za, w2b, b2b, nw8, o_f, reverse=True, **gla_kw)

        sink_tab = jnp.broadcast_to(
            jnp.pad(swa_sink[l].reshape(SWA_KV_HEADS, SWA_GROUP), ((0, 0), (0, SUBLANE - SWA_GROUP)))[:, :, None],
            (SWA_KV_HEADS, SUBLANE, LANE))
        swa_o = _swa(zb, sink_tab, batch=batch, seq=seq, n_ctx_seq=n_ctx_seq, with_ctx=not last)

        ln8 = _pad_rows(jnp.stack([ln_g[l, 0], ln_b[l, 0], ln_g[l, 1], ln_b[l, 1]]), SUBLANE)
        x1 = _outproj(conv_o, gla_o, swa_o, xs, mods[l], w_out[l].astype(BF16), ln8, seg_of, rows, alpha)

        hb, e1, e2, th = _peer_query(x1, mods[l], peer_wq[l].T.astype(BF16), peer_subkeys[l], seg_of, rows)
        xs = _peer_dense(hb, e1, e2, th, peer_u[l].astype(BF16), peer_v[l].T.astype(BF16), x1, mods[l], ln8,
                         seg_of, rows, alpha)
    return xs[:n_lat].reshape(batch, seq, d)
```
